```python
import math
import jax, jax.numpy as jnp
from jax import lax
import numpy as np

D_MODEL = 1024
BATCH = 2
SEQ = 8192
DEPTH = 4

N_MIXERS = 2
N_SSD_LAYERS = (DEPTH + 1) // 2
N_HGRN_LAYERS = DEPTH // 2

SSD_EXPAND = 2
SSD_D_INNER = SSD_EXPAND * D_MODEL
SSD_HEAD_DIM = 64
SSD_N_HEADS = SSD_D_INNER // SSD_HEAD_DIM
SSD_N_GROUPS = 4
SSD_HEADS_PER_GROUP = SSD_N_HEADS // SSD_N_GROUPS
SSD_D_STATE = 128
SSD_CONV_WIDTH = 4
SSD_CHUNK = 128
SSD_BC_DIM = SSD_N_GROUPS * SSD_D_STATE
SSD_CONV_DIM = SSD_D_INNER + 2 * SSD_BC_DIM
SSD_IN_DIM = SSD_D_INNER + SSD_CONV_DIM + SSD_N_HEADS

HGRN_EXPAND = 128
HGRN_N_HEADS = D_MODEL // HGRN_EXPAND
HGRN_HEAD_V = D_MODEL // HGRN_N_HEADS
HGRN_CHUNK = 64
HGRN_IN_DIM = 4 * D_MODEL

D_FF = 4 * D_MODEL
PLE_DIM = 256
DEEPNORM_ALPHA = (2.0 * DEPTH) ** 0.25
DEEPNORM_BETA = (8.0 * DEPTH) ** -0.25
LN_EPS = 1e-5
RMS_EPS = 1e-5

kernel_name = "hybrid_ssd_hgrn2_deepnorm_trunk"


def layer_norm(x, g, b):
    xf = x.astype(jnp.float32)
    mu = jnp.mean(xf, axis=-1, keepdims=True)
    xc = xf - mu
    var = jnp.mean(xc * xc, axis=-1, keepdims=True)
    y = xc * lax.rsqrt(var + LN_EPS) * g.astype(jnp.float32) + b.astype(jnp.float32)
    return y.astype(x.dtype)


def rms_norm(x, w):
    xf = x.astype(jnp.float32)
    return xf * lax.rsqrt(jnp.mean(xf * xf, axis=-1, keepdims=True) + RMS_EPS) * w.astype(jnp.float32)


def causal_depthwise_conv(u, w, b):
    k, c = w.shape
    out = lax.conv_general_dilated(u, w[:, None, :].astype(u.dtype), window_strides=(1,),
                                   padding=[(k - 1, 0)], dimension_numbers=('NWC', 'WIO', 'NWC'),
                                   feature_group_count=c)
    return out + b


def segsum_exp(a):
    t = a.shape[-1]
    cs = jnp.cumsum(a, axis=-1)
    seg = cs[..., :, None] - cs[..., None, :]
    mask = jnp.tril(jnp.ones((t, t), dtype=bool))
    return jnp.exp(jnp.where(mask, seg, -jnp.inf))


def ssd_mixer(u, w_in, conv_w, conv_b, dt_bias, a_log, d_skip, norm_w, w_out):
    bsz, seq, _ = u.shape
    nc = seq // SSD_CHUNK
    g_, r_, p_, n_ = SSD_N_GROUPS, SSD_HEADS_PER_GROUP, SSD_HEAD_DIM, SSD_D_STATE
    zxbcdt = u @ w_in
    z, xbc, dt = jnp.split(zxbcdt, [SSD_D_INNER, SSD_D_INNER + SSD_CONV_DIM], axis=-1)
    xbc = jax.nn.silu(causal_depthwise_conv(xbc, conv_w, conv_b))
    xs, bm, cm = jnp.split(xbc, [SSD_D_INNER, SSD_D_INNER + SSD_BC_DIM], axis=-1)
    xs = xs.astype(jnp.float32).reshape(bsz, nc, SSD_CHUNK, g_, r_, p_)
    bm = bm.astype(jnp.float32).reshape(bsz, nc, SSD_CHUNK, g_, n_)
    cm = cm.astype(jnp.float32).reshape(bsz, nc, SSD_CHUNK, g_, n_)
    dt = jax.nn.softplus(dt.astype(jnp.float32) + dt_bias.astype(jnp.float32))
    a = -jnp.exp(a_log.astype(jnp.float32))
    dt = dt.reshape(bsz, nc, SSD_CHUNK, g_, r_)
    da = dt * a.reshape(g_, r_)
    xdt = xs * dt[..., None]
    a_cs = jnp.cumsum(da, axis=2)
    decay = segsum_exp(jnp.moveaxis(da, 2, -1))
    cb = jnp.einsum('bclgn,bcsgn->bcgls', cm, bm)
    y_diag = jnp.einsum('bcgls,bcgrls,bcsgrp->bclgrp', cb, decay, xdt)
    decay_to_end = jnp.exp(a_cs[:, :, -1:] - a_cs)
    states = jnp.einsum('bclgn,bclgr,bclgrp->bcgrpn', bm, decay_to_end, xdt)
    chunk_decay = jnp.exp(a_cs[:, :, -1])

    def step(h, inp):
        st, dec = inp
        return h * dec[..., None, None] + st, h

    h0 = jnp.zeros((bsz, g_, r_, p_, n_), jnp.float32)
    _, prev = lax.scan(step, h0, (jnp.moveaxis(states, 1, 0), jnp.moveaxis(chunk_decay, 1, 0)))
    prev = jnp.moveaxis(prev, 0, 1)
    y_off = jnp.einsum('bclgn,bcgrpn,bclgr->bclgrp', cm, prev, jnp.exp(a_cs))
    y = y_diag + y_off + d_skip.astype(jnp.float32).reshape(g_, r_)[:, :, None] * xs
    y = y.reshape(bsz, seq, SSD_D_INNER) * jax.nn.silu(z.astype(jnp.float32))
    y = rms_norm(y.reshape(bsz, seq, g_, SSD_D_INNER // g_), norm_w.reshape(g_, SSD_D_INNER // g_))
    y = y.reshape(bsz, seq, SSD_D_INNER).astype(u.dtype)
    return y @ w_out


def hgrn2_mixer(u, w_in, lb, norm_w, w_out):
    bsz, seq, _ = u.shape
    nc = seq // HGRN_CHUNK
    h_, k_dim, v_dim, c_ = HGRN_N_HEADS, HGRN_EXPAND, HGRN_HEAD_V, HGRN_CHUNK
    q, fz, v, g = jnp.split(u @ w_in, 4, axis=-1)
    fz = fz.astype(jnp.float32)
    lb = lb.astype(jnp.float32)
    log_f = jnp.logaddexp(jnp.log(lb), jnp.log1p(-lb) + jax.nn.log_sigmoid(fz))
    k = (1.0 - lb) * jax.nn.sigmoid(-fz)

    def chunked(t, d):
        return jnp.moveaxis(t.astype(jnp.float32).reshape(bsz, nc, c_, h_, d), 1, 0)

    qc, kc, vc = chunked(q, k_dim), chunked(k, k_dim), chunked(v, v_dim)
    b_cs = jnp.cumsum(chunked(log_f, k_dim), axis=2)
    mask = jnp.tril(jnp.ones((c_, c_), dtype=bool))[None, :, :, None, None]

    def step(s_prev, inp):
        q_, kk, v_, b_ = inp
        diff = b_[:, :, None] - b_[:, None, :]
        dec = jnp.exp(jnp.where(mask, diff, -jnp.inf))
        scores = jnp.einsum('blhk,bshk,blshk->bhls', q_, kk, dec)
        o = jnp.einsum('bhls,bshv->blhv', scores, v_) + jnp.einsum('blhk,bhkv->blhv', q_ * jnp.exp(b_), s_prev)
        b_last = b_[:, -1]
        s_new = jnp.exp(b_last)[..., None] * s_prev + jnp.einsum(
            'bshk,bshv->bhkv', kk * jnp.exp(b_last[:, None] - b_), v_)
        return s_new, o

    s0 = jnp.zeros((bsz, h_, k_dim, v_dim), jnp.float32)
    _, o = lax.scan(step, s0, (qc, kc, vc, b_cs))
    o = jnp.moveaxis(o, 0, 1).reshape(bsz, seq, h_, v_dim)
    o = rms_norm(o, norm_w).reshape(bsz, seq, D_MODEL) * jax.nn.silu(g.astype(jnp.float32))
    return o.astype(u.dtype) @ w_out


def sq_relu_mlp(h, w1, w2):
    a = jax.nn.relu(h @ w1)
    return (a * a) @ w2


def setup_inputs(seed: int = 0) -> dict:
    key = jax.random.key(seed)
    ks = jax.random.split(key, 20)
    f32 = jnp.float32
    nrm = lambda k, shape, s: jax.random.normal(k, shape, f32) * s
    x = jax.random.normal(ks[0], (BATCH, SEQ, D_MODEL), f32)
    p = jax.random.normal(ks[1], (DEPTH, BATCH, SEQ, PLE_DIM), f32)
    ssd_w_in = nrm(ks[2], (N_SSD_LAYERS, D_MODEL, SSD_IN_DIM), D_MODEL ** -0.5)
    ssd_conv_w = nrm(ks[3], (N_SSD_LAYERS, SSD_CONV_WIDTH, SSD_CONV_DIM), SSD_CONV_WIDTH ** -0.5)
    ssd_conv_b = nrm(ks[4], (N_SSD_LAYERS, SSD_CONV_DIM), 0.02)
    dt0 = jnp.exp(jax.random.uniform(ks[5], (N_SSD_LAYERS, SSD_N_HEADS), f32,
                                     math.log(1e-3), math.log(1e-1)))
    ssd_dt_bias = dt0 + jnp.log(-jnp.expm1(-dt0))
    ssd_a_log = jnp.log(jax.random.uniform(ks[6], (N_SSD_LAYERS, SSD_N_HEADS), f32, 1.0, 16.0))
    ssd_d = 1.0 + nrm(ks[7], (N_SSD_LAYERS, SSD_N_HEADS), 0.02)
    ssd_norm_w = 1.0 + nrm(ks[8], (N_SSD_LAYERS, SSD_D_INNER), 0.02)
    ssd_w_out = nrm(ks[9], (N_SSD_LAYERS, SSD_D_INNER, D_MODEL), SSD_D_INNER ** -0.5 * DEEPNORM_BETA)
    hgrn_w_in = nrm(ks[10], (N_HGRN_LAYERS, D_MODEL, HGRN_IN_DIM), D_MODEL ** -0.5)
    hgrn_lower_bounds = nrm(ks[11], (N_HGRN_LAYERS, D_MODEL), 0.1)
    hgrn_norm_w = 1.0 + nrm(ks[12], (N_HGRN_LAYERS, HGRN_HEAD_V), 0.02)
    hgrn_w_out = nrm(ks[13], (N_HGRN_LAYERS, D_MODEL, D_MODEL), D_MODEL ** -0.5 * DEEPNORM_BETA)
    ln_g = 1.0 + nrm(ks[14], (DEPTH, 2, D_MODEL), 0.02)
    ln_b = nrm(ks[15], (DEPTH, 2, D_MODEL), 0.02)
    mlp_w1 = nrm(ks[16], (DEPTH, D_MODEL, D_FF), D_MODEL ** -0.5)
    mlp_w2 = nrm(ks[17], (DEPTH, D_FF, D_MODEL), D_FF ** -0.5 * DEEPNORM_BETA)
    ple_w_proj = nrm(ks[18], (DEPTH, PLE_DIM, D_MODEL), PLE_DIM ** -0.5)
    ple_w_gate = nrm(ks[19], (DEPTH, D_MODEL, D_MODEL), D_MODEL ** -0.5)
    return {"x": x, "p": p, "ssd_w_in": ssd_w_in, "ssd_conv_w": ssd_conv_w, "ssd_conv_b": ssd_conv_b,
            "ssd_dt_bias": ssd_dt_bias, "ssd_a_log": ssd_a_log, "ssd_d": ssd_d, "ssd_norm_w": ssd_norm_w,
            "ssd_w_out": ssd_w_out, "hgrn_w_in": hgrn_w_in, "hgrn_lower_bounds": hgrn_lower_bounds,
            "hgrn_norm_w": hgrn_norm_w, "hgrn_w_out": hgrn_w_out, "ln_g": ln_g, "ln_b": ln_b,
            "mlp_w1": mlp_w1, "mlp_w2": mlp_w2, "ple_w_proj": ple_w_proj, "ple_w_gate": ple_w_gate}


def reference(x, p, ssd_w_in, ssd_conv_w, ssd_conv_b, ssd_dt_bias, ssd_a_log, ssd_d, ssd_norm_w,
              ssd_w_out, hgrn_w_in, hgrn_lower_bounds, hgrn_norm_w, hgrn_w_out, ln_g, ln_b,
              mlp_w1, mlp_w2, ple_w_proj, ple_w_gate):
    lbs = jnp.cumsum(jax.nn.softmax(hgrn_lower_bounds.astype(jnp.float32), axis=0), axis=0)
    lbs = lbs - lbs[0:1]
    for i in range(DEPTH):
        j = i // N_MIXERS
        if i % N_MIXERS == 0:
            mix = ssd_mixer(x, ssd_w_in[j], ssd_conv_w[j], ssd_conv_b[j], ssd_dt_bias[j], ssd_a_log[j],
                            ssd_d[j], ssd_norm_w[j], ssd_w_out[j])
        else:
            mix = hgrn2_mixer(x, hgrn_w_in[j], lbs[j], hgrn_norm_w[j], hgrn_w_out[j])
        h = layer_norm(DEEPNORM_ALPHA * x + mix, ln_g[i, 0], ln_b[i, 0])
        h = layer_norm(DEEPNORM_ALPHA * h + sq_relu_mlp(h, mlp_w1[i], mlp_w2[i]), ln_g[i, 1], ln_b[i, 1])
        x = h + jax.nn.sigmoid(h @ ple_w_gate[i]) * (p[i] @ ple_w_proj[i])
    return x
```

```python
import functools

import jax
import jax.numpy as jnp
from jax import lax
from jax.experimental import pallas as pl
from jax.experimental.pallas import tpu as pltpu

F32 = jnp.float32
BF16 = jnp.bfloat16

SSD_N_GROUPS = 4
SSD_D_STATE = 128
SSD_HEAD_DIM = 64
SSD_CHUNK = 128
HGRN_HEAD_K = 128
HGRN_CHUNK = 64
HGRN_SUB = 16
LN_EPS = 1e-5
RMS_EPS = 1e-5

V7X_VMEM_BYTES = 64 * 1024 * 1024
VMEM_LIMIT = V7X_VMEM_BYTES - 8 * 1024 * 1024

PROJ_ROWS = 256
CORE_ROWS = 512
POST_ROWS = 256


def _bf(x):
    return x.astype(BF16)


def _dot(a, b):
    return jnp.dot(_bf(a), _bf(b), preferred_element_type=F32)


def _dot_nt(a, b):
    return lax.dot_general(_bf(a), _bf(b), (((1,), (1,)), ((), ())), preferred_element_type=F32)


def _dot_tn(a, b):
    return lax.dot_general(_bf(a), _bf(b), (((0,), (0,)), ((), ())), preferred_element_type=F32)


def _split3(x):
    hi = _bf(x)
    r1 = x - hi.astype(F32)
    mid = _bf(r1)
    lo = _bf(r1 - mid.astype(F32))
    return hi, mid, lo


def _cumsum_rows(tri, x):
    hi, mid, lo = _split3(x)
    t = _bf(tri)
    d = functools.partial(jnp.dot, preferred_element_type=F32)
    return d(t, hi) + d(t, mid) + d(t, lo)


def _cumsum_cols(x, triu):
    hi, mid, lo = _split3(x)
    t = _bf(triu)
    d = functools.partial(jnp.dot, preferred_element_type=F32)
    return d(hi, t) + d(mid, t) + d(lo, t)


def _sigmoid(x):
    return 1.0 / (1.0 + jnp.exp(-x))


def _silu(x):
    return x * _sigmoid(x)


def _softplus(x):
    return jnp.maximum(x, 0.0) + jnp.log1p(jnp.exp(-jnp.abs(x)))


def _layer_norm(x, g, b):
    mu = jnp.mean(x, axis=-1, keepdims=True)
    xc = x - mu
    var = jnp.mean(xc * xc, axis=-1, keepdims=True)
    return xc * lax.rsqrt(var + LN_EPS) * g + b


def _const_spec(shape):
    nd = len(shape)
    return pl.BlockSpec(shape, lambda *_: (0,) * nd, pipeline_mode=pl.Buffered(1))


def _params(n_grid):
    return pltpu.CompilerParams(dimension_semantics=("arbitrary",) * n_grid, vmem_limit_bytes=VMEM_LIMIT)


def _ssd_proj_kernel(x_ref, wzx_ref, wdt_ref, wdtT_ref, z_ref, xbc_ref, dt_ref, dtT_ref, *, d_inner, n_groups):
    xb = _bf(x_ref[...])
    rows = xb.shape[0]
    step = 512
    for c in range(0, z_ref.shape[1], step):
        z_ref[:, c:c + step] = jnp.dot(xb, wzx_ref[:, c:c + step], preferred_element_type=F32)
    for c in range(0, xbc_ref.shape[1], step):
        xbc_ref[:, c:c + step] = jnp.dot(xb, wzx_ref[:, d_inner + c:d_inner + c + step], preferred_element_type=F32)
    dt = jnp.dot(xb, wdt_ref[...], preferred_element_type=F32)
    hpg = dt.shape[1] // n_groups
    for g in range(n_groups):
        dt_ref[g] = dt[:, g * hpg:(g + 1) * hpg]
    for c in range(rows // SSD_CHUNK):
        xc = xb[c * SSD_CHUNK:(c + 1) * SSD_CHUNK]
        dtT_ref[c] = lax.dot_general(wdtT_ref[...], xc, (((1,), (1,)), ((), ())), preferred_element_type=F32)


def _ssd_proj(x2, wzx, wdt, wdtT, d_inner, conv_dim, n_groups):
    t, d = x2.shape
    n_heads = wdt.shape[1]
    tm = PROJ_ROWS
    kern = functools.partial(_ssd_proj_kernel, d_inner=d_inner, n_groups=n_groups)
    return pl.pallas_call(
        kern,
        grid=(t // tm,),
        in_specs=[pl.BlockSpec((tm, d), lambda i: (i, 0)),
                  _const_spec(wzx.shape), _const_spec(wdt.shape), _const_spec(wdtT.shape)],
        out_specs=[pl.BlockSpec((tm, d_inner), lambda i: (i, 0)),
                   pl.BlockSpec((tm, conv_dim), lambda i: (i, 0)),
                   pl.BlockSpec((n_groups, tm, n_heads // n_groups), lambda i: (0, i, 0)),
                   pl.BlockSpec((tm // SSD_CHUNK, n_heads, SSD_CHUNK), lambda i: (i, 0, 0))],
        out_shape=[jax.ShapeDtypeStruct((t, d_inner), F32),
                   jax.ShapeDtypeStruct((t, conv_dim), F32),
                   jax.ShapeDtypeStruct((n_groups, t, n_heads // n_groups), F32),
                   jax.ShapeDtypeStruct((t // SSD_CHUNK, n_heads, SSD_CHUNK), F32)],
        compiler_params=_params(1),
        name="ssd_in_proj",
    )(x2, wzx, wdt, wdtT)


def _proj_kernel(x_ref, w_ref, o_ref):
    xb = _bf(x_ref[...])
    step = 512
    for c in range(0, o_ref.shape[1], step):
        o_ref[:, c:c + step] = jnp.dot(xb, w_ref[:, c:c + step], preferred_element_type=F32)


def _proj(x2, w):
    t, d = x2.shape
    n = w.shape[1]
    tm = PROJ_ROWS
    return pl.pallas_call(
        _proj_kernel,
        grid=(t // tm,),
        in_specs=[pl.BlockSpec((tm, d), lambda i: (i, 0)), _const_spec(w.shape)],
        out_specs=pl.BlockSpec((tm, n), lambda i: (i, 0)),
        out_shape=jax.ShapeDtypeStruct((t, n), F32),
        compiler_params=_params(1),
        name="hgrn_in_proj",
    )(x2, w)


def _ssd_core_kernel(xs_ref, b_ref, c_ref, z_ref, dt_ref, dtT_ref,
                     cwx_ref, cwb_ref, cwc_ref, cbx_ref, cbb_ref, cbc_ref,
                     dtb_ref, dtbT_ref, alog_ref, alogT_ref, dsk_ref, nw_ref,
                     o_ref,
                     bufx, bufb, bufc, actx, actb, actc, ybuf, state):
    rows = xs_ref.shape[0]
    q = SSD_CHUNK
    hpg = state.shape[0]
    p_dim = state.shape[2]
    kw = cwx_ref.shape[0]
    halo = 8

    @pl.when(pl.program_id(2) == 0)
    def _():
        bufx[0:halo, :] = jnp.zeros((halo, bufx.shape[1]), F32)
        bufb[0:halo, :] = jnp.zeros((halo, bufb.shape[1]), F32)
        bufc[0:halo, :] = jnp.zeros((halo, bufc.shape[1]), F32)
        state[...] = jnp.zeros(state.shape, F32)

    def conv(src_ref, buf, w_ref, bias_ref, dst):
        buf[halo:halo + rows, :] = src_ref[...]
        acc = bias_ref[...] + w_ref[kw - 1:kw, :] * buf[halo:halo + rows, :]
        for k in range(kw - 1):
            off = halo - (kw - 1) + k
            acc = acc + w_ref[k:k + 1, :] * buf[off:off + rows, :]
        dst[...] = _silu(acc)
        buf[0:halo, :] = buf[rows:rows + halo, :]

    conv(xs_ref, bufx, cwx_ref, cbx_ref, actx)
    conv(b_ref, bufb, cwb_ref, cbb_ref, actb)
    conv(c_ref, bufc, cwc_ref, cbc_ref, actc)

    li = lax.broadcasted_iota(jnp.int32, (q, q), 0)
    si = lax.broadcasted_iota(jnp.int32, (q, q), 1)
    causal = li >= si
    tril = causal.astype(F32)
    triu = (li <= si).astype(F32)
    a_row = -jnp.exp(alog_ref[...])
    a_col = -jnp.exp(alogT_ref[...])
    dskip = dsk_ref[...]

    def chunk(c, carry):
        r0 = pl.multiple_of(c * q, q)
        xs = actx[pl.ds(r0, q), :]
        bg = actb[pl.ds(r0, q), :]
        cg = actc[pl.ds(r0, q), :]
        dt = _softplus(dt_ref[pl.ds(r0, q), :] + dtb_ref[...])
        dtT = _softplus(dtT_ref[c] + dtbT_ref[...])
        acs = _cumsum_rows(tril, dt * a_row)
        acsT = _cumsum_cols(dtT * a_col, triu)
        last = acsT[:, q - 1:q]
        wT = dtT * jnp.exp(last - acsT)
        cdT = jnp.exp(last)
        ea = jnp.exp(acs)
        cb = _dot_nt(cg, bg)
        bgT = bg.T
        for r in range(hpg):
            col = acs[:, r:r + 1]
            row = acsT[r:r + 1, :]
            decay = jnp.where(causal, jnp.exp(col - row), 0.0) * dtT[r:r + 1, :]
            xh = xs[:, r * p_dim:(r + 1) * p_dim]
            s_old = state[r]
            y = _dot(cb * decay, xh) + ea[:, r:r + 1] * _dot(cg, s_old) + dskip[:, r:r + 1] * xh
            state[r] = cdT[r:r + 1, :] * s_old + _dot(bgT * wT[r:r + 1, :], xh)
            ybuf[pl.ds(r0, q), r * p_dim:(r + 1) * p_dim] = y
        return carry

    lax.fori_loop(0, rows // q, chunk, 0)

    y = ybuf[...] * _silu(z_ref[...])
    ms = jnp.mean(y * y, axis=-1, keepdims=True)
    o_ref[...] = y * lax.rsqrt(ms + RMS_EPS) * nw_ref[...]


def _ssd_core(z, xbc, dt, dtT, conv_w, conv_b, dt_bias, a_log, d_skip, norm_w, batch, seq):
    t, d_inner = z.shape
    g_ = SSD_N_GROUPS
    n_ = SSD_D_STATE
    n_heads = dt_bias.shape[0]
    hpg = n_heads // g_
    gw = d_inner // g_
    p_dim = gw // hpg
    rows = CORE_ROWS
    nt = seq // rows
    kw = conv_w.shape[0]
    xoff, boff, coff = 0, d_inner // n_, (d_inner + g_ * n_) // n_
    gpn = gw // n_
    assert gw % n_ == 0

    row_blk = lambda b, g, j: b * nt + j
    conv_b2 = conv_b.reshape(1, -1)
    dtb = dt_bias.reshape(g_, 1, hpg)
    dtbT = dt_bias.reshape(g_, hpg, 1)
    alog = a_log.reshape(g_, 1, hpg)
    alogT = a_log.reshape(g_, hpg, 1)
    dsk = d_skip.reshape(g_, 1, hpg)
    nw = norm_w.reshape(1, -1)

    in_specs = [
        pl.BlockSpec((rows, gw), lambda b, g, j: (row_blk(b, g, j), g)),
        pl.BlockSpec((rows, n_), lambda b, g, j: (row_blk(b, g, j), boff + g)),
        pl.BlockSpec((rows, n_), lambda b, g, j: (row_blk(b, g, j), coff + g)),
        pl.BlockSpec((rows, gw), lambda b, g, j: (row_blk(b, g, j), g)),
        pl.BlockSpec((None, rows, hpg), lambda b, g, j: (g, row_blk(b, g, j), 0)),
        pl.BlockSpec((rows // SSD_CHUNK, hpg, SSD_CHUNK), lambda b, g, j: (row_blk(b, g, j), g, 0)),
        pl.BlockSpec((kw, gw), lambda b, g, j: (0, g)),
        pl.BlockSpec((kw, n_), lambda b, g, j: (0, boff + g)),
        pl.BlockSpec((kw, n_), lambda b, g, j: (0, coff + g)),
        pl.BlockSpec((1, gw), lambda b, g, j: (0, g)),
        pl.BlockSpec((1, n_), lambda b, g, j: (0, boff + g)),
        pl.BlockSpec((1, n_), lambda b, g, j: (0, coff + g)),
        pl.BlockSpec((None, 1, hpg), lambda b, g, j: (g, 0, 0)),
        pl.BlockSpec((None, hpg, 1), lambda b, g, j: (g, 0, 0)),
        pl.BlockSpec((None, 1, hpg), lambda b, g, j: (g, 0, 0)),
        pl.BlockSpec((None, hpg, 1), lambda b, g, j: (g, 0, 0)),
        pl.BlockSpec((None, 1, hpg), lambda b, g, j: (g, 0, 0)),
        pl.BlockSpec((1, gw), lambda b, g, j: (0, g)),
    ]
    del xoff, gpn
    halo = 8
    scratch = [
        pltpu.VMEM((rows + halo, gw), F32), pltpu.VMEM((rows + halo, n_), F32), pltpu.VMEM((rows + halo, n_), F32),
        pltpu.VMEM((rows, gw), F32), pltpu.VMEM((rows, n_), F32), pltpu.VMEM((rows, n_), F32),
        pltpu.VMEM((rows, gw), F32),
        pltpu.VMEM((hpg, n_, p_dim), F32),
    ]
    return pl.pallas_call(
        _ssd_core_kernel,
        grid=(batch, g_, nt),
        in_specs=in_specs,
        out_specs=pl.BlockSpec((rows, gw), lambda b, g, j: (row_blk(b, g, j), g)),
        out_shape=jax.ShapeDtypeStruct((t, d_inner), F32),
        scratch_shapes=scratch,
        compiler_params=_params(3),
        name="ssd_core",
    )(xbc, xbc, xbc, z, dt, dtT, conv_w, conv_w, conv_w, conv_b2, conv_b2, conv_b2,
      dtb, dtbT, alog, alogT, dsk, nw)


def _hgrn_core_kernel(q_ref, f_ref, v_ref, g_ref, lb_ref, nw_ref, o_ref, b_s, kk_s, state):
    rows = q_ref.shape[0]
    cs = HGRN_CHUNK
    sb = HGRN_SUB
    half = sb // 2
    kd = q_ref.shape[1]

    @pl.when(pl.program_id(2) == 0)
    def _():
        state[...] = jnp.zeros(state.shape, F32)

    li = lax.broadcasted_iota(jnp.int32, (cs, cs), 0)
    si = lax.broadcasted_iota(jnp.int32, (cs, cs), 1)
    tril = (li >= si).astype(F32)
    key_row = lax.broadcasted_iota(jnp.int32, (cs, kd), 0)
    sub_row = lax.broadcasted_iota(jnp.int32, (half, 1), 0)
    lb = lb_ref[...]
    log_lb = jnp.log(lb)
    log_1m_lb = jnp.log1p(-lb)
    one_m_lb = 1.0 - lb

    def chunk(c, carry):
        r0 = pl.multiple_of(c * cs, cs)
        q = q_ref[pl.ds(r0, cs), :]
        fz = f_ref[pl.ds(r0, cs), :]
        v = v_ref[pl.ds(r0, cs), :]
        e = jnp.exp(-jnp.abs(fz))
        log_sig = jnp.minimum(fz, 0.0) - jnp.log1p(e)
        t2 = log_1m_lb + log_sig
        log_f = jnp.maximum(log_lb, t2) + jnp.log1p(jnp.exp(-jnp.abs(log_lb - t2)))
        kk = one_m_lb * (jnp.where(fz >= 0.0, e, 1.0) / (1.0 + e))
        b = _cumsum_rows(tril, log_f)
        b_s[...] = b
        kk_s[...] = kk

        st = state[...]
        o = _dot_nt(q * jnp.exp(b), st)
        b_last = b[cs - 1:cs, :]
        state[...] = jnp.exp(b_last) * st + _dot_tn(v, kk * jnp.exp(b_last - b))

        sc_rows = [jnp.zeros((sb, cs), F32)]
        for i in range(1, cs // sb):
            ref_row = b[i * sb:i * sb + 1, :]
            qp = q[i * sb:(i + 1) * sb] * jnp.exp(b[i * sb:(i + 1) * sb] - ref_row)
            kp = kk * jnp.exp(jnp.where(key_row < i * sb, ref_row - b, -jnp.inf))
            sc_rows.append(_dot_nt(qp, kp))
        o = o + _dot(jnp.concatenate(sc_rows, axis=0), v)

        diag = []
        for i in range(cs // sb):
            base = i * sb
            q_t, q_b = q[base:base + half], q[base + half:base + sb]
            b_t, b_b = b[base:base + half], b[base + half:base + sb]
            acc_t = jnp.zeros((half, kd), F32)
            acc_b = jnp.zeros((half, kd), F32)
            for s in range(sb):
                bs_ = jnp.broadcast_to(b_s[base + s:base + s + 1, :], (half, kd))
                ks_ = jnp.broadcast_to(kk_s[base + s:base + s + 1, :], (half, kd))
                vs_ = jnp.broadcast_to(v_ref[pl.ds(r0 + base + s, 1), :], (half, kd))
                if s < half:
                    red = jnp.sum(q_t * jnp.exp(b_t - bs_) * ks_, axis=-1, keepdims=True)
                    acc_t = acc_t + jnp.where(sub_row >= s, red, 0.0) * vs_
                    red = jnp.sum(q_b * jnp.exp(b_b - bs_) * ks_, axis=-1, keepdims=True)
                    acc_b = acc_b + red * vs_
                else:
                    red = jnp.sum(q_b * jnp.exp(b_b - bs_) * ks_, axis=-1, keepdims=True)
                    acc_b = acc_b + jnp.where(sub_row >= s - half, red, 0.0) * vs_
            diag += [acc_t, acc_b]
        o = o + jnp.concatenate(diag, axis=0)

        ms = jnp.mean(o * o, axis=-1, keepdims=True)
        o = o * lax.rsqrt(ms + RMS_EPS) * nw_ref[...]
        o_ref[pl.ds(r0, cs), :] = o * _silu(g_ref[pl.ds(r0, cs), :])
        return carry

    lax.fori_loop(0, rows // cs, chunk, 0)


def _hgrn_core(qfvg, lb, norm_w, batch, seq):
    t, n4 = qfvg.shape
    d = n4 // 4
    kd = HGRN_HEAD_K
    n_heads = d // kd
    rows = CORE_ROWS
    nt = seq // rows
    row_blk = lambda b, h, j: b * nt + j
    blk = lambda part: pl.BlockSpec((rows, kd), lambda b, h, j: (row_blk(b, h, j), part * n_heads + h))
    return pl.pallas_call(
        _hgrn_core_kernel,
        grid=(batch, n_heads, nt),
        in_specs=[blk(0), blk(1), blk(2), blk(3),
                  pl.BlockSpec((1, kd), lambda b, h, j: (0, h)),
                  pl.BlockSpec((1, kd), lambda b, h, j: (0, 0))],
        out_specs=pl.BlockSpec((rows, kd), lambda b, h, j: (row_blk(b, h, j), h)),
        out_shape=jax.ShapeDtypeStruct((t, d), F32),
        scratch_shapes=[pltpu.VMEM((HGRN_CHUNK, kd), F32), pltpu.VMEM((HGRN_CHUNK, kd), F32),
                        pltpu.VMEM((kd, kd), F32)],
        compiler_params=_params(3),
        name="hgrn_core",
    )(qfvg, qfvg, qfvg, qfvg, lb.reshape(1, d), norm_w.reshape(1, kd))


def _post_kernel(y_ref, x_ref, p_ref, wo_ref, lng_ref, lnb_ref, w1_ref, w2_ref, wg_ref, wp_ref, o_ref, *, alpha):
    mix = _dot(y_ref[...], wo_ref[...])
    h = _layer_norm(alpha * x_ref[...] + mix, lng_ref[0:1, :], lnb_ref[0:1, :])
    a = jnp.maximum(_dot(h, w1_ref[...]), 0.0)
    m = _dot(a * a, w2_ref[...])
    h2 = _layer_norm(alpha * h + m, lng_ref[1:2, :], lnb_ref[1:2, :])
    gate = _sigmoid(_dot(h2, wg_ref[...]))
    o_ref[...] = h2 + gate * _dot(p_ref[...], wp_ref[...])


def _post(y, x2, p2, layer, w_out, ln_g, ln_b, w1, w2, wg, wp, alpha):
    t, d = x2.shape
    kin = y.shape[1]
    pd = p2.shape[1]
    tm = POST_ROWS
    nblk = t // tm
    return pl.pallas_call(
        functools.partial(_post_kernel, alpha=alpha),
        grid=(nblk,),
        in_specs=[pl.BlockSpec((tm, kin), lambda i: (i, 0)),
                  pl.BlockSpec((tm, d), lambda i: (i, 0)),
                  pl.BlockSpec((tm, pd), lambda i: (layer * nblk + i, 0)),
                  _const_spec(w_out.shape), _const_spec(ln_g.shape), _const_spec(ln_b.shape),
                  _const_spec(w1.shape), _const_spec(w2.shape), _const_spec(wg.shape), _const_spec(wp.shape)],
        out_specs=pl.BlockSpec((tm, d), lambda i: (i, 0)),
        out_shape=jax.ShapeDtypeStruct((t, d), F32),
        compiler_params=_params(1),
        name="post",
    )(y, x2, p2, w_out, ln_g, ln_b, w1, w2, wg, wp)


def kernel(x, p, ssd_w_in, ssd_conv_w, ssd_conv_b, ssd_dt_bias, ssd_a_log, ssd_d, ssd_norm_w, ssd_w_out, hgrn_w_in, hgrn_lower_bounds, hgrn_norm_w, hgrn_w_out, ln_g, ln_b, mlp_w1, mlp_w2, ple_w_proj, ple_w_gate):
    batch, seq, d = x.shape
    depth = ln_g.shape[0]
    t = batch * seq
    alpha = (2.0 * depth) ** 0.25
    d_inner = ssd_w_out.shape[1]
    conv_dim = ssd_conv_w.shape[-1]
    assert seq % CORE_ROWS == 0 and t % PROJ_ROWS == 0 and t % POST_ROWS == 0

    lbs = jnp.cumsum(jax.nn.softmax(hgrn_lower_bounds.astype(F32), axis=0), axis=0)
    lbs = lbs - lbs[0:1]

    x2 = x.reshape(t, d)
    p2 = p.reshape(depth * t, p.shape[-1])
    for i in range(depth):
        j = i // 2
        if i % 2 == 0:
            w_in = ssd_w_in[j]
            wzx = _bf(w_in[:, :d_inner + conv_dim])
            wdt = _bf(w_in[:, d_inner + conv_dim:])
            z, xbc, dt, dtT = _ssd_proj(x2, wzx, wdt, wdt.T, d_inner, conv_dim, SSD_N_GROUPS)
            y = _ssd_core(z, xbc, dt, dtT, ssd_conv_w[j], ssd_conv_b[j], ssd_dt_bias[j], ssd_a_log[j],
                          ssd_d[j], ssd_norm_w[j], batch, seq)
            w_out = _bf(ssd_w_out[j])
        else:
            qfvg = _proj(x2, _bf(hgrn_w_in[j]))
            y = _hgrn_core(qfvg, lbs[j], hgrn_norm_w[j], batch, seq)
            w_out = _bf(hgrn_w_out[j])
        x2 = _post(y, x2, p2, i, w_out, ln_g[i], ln_b[i], _bf(mlp_w1[i]), _bf(mlp_w2[i]),
                   _bf(ple_w_gate[i]), _bf(ple_w_proj[i]), alpha)
    return x2.reshape(batch, seq, d)
```
